```python
import math
import jax, jax.numpy as jnp
from jax import lax
import numpy as np

D_MODEL = 1024
BATCH = 16
SEQ = 256
DEPTH = 2
DEC_BATCH = 4
DEC_SEQ = 2048
PAST_LEN = 512

GRID_W = 64
EPS = 1e-6
GLA_HEADS = 4
GLA_DK = 64
GLA_DV = 128
GLA_RANK = 16
GLA_TAU = 16.0
GLA_CHUNK = 64
N_HEADS = 8
N_KV_HEADS = 2
HEAD_DIM = 64
ROPE_THETA = 10000.0
Q_BLOCK = 128
S5_WIDTH = 512
S5_GROUP = 16
S5_GROUPS = S5_WIDTH // S5_GROUP
S5_STATE = 64
S5_DT_MIN = 0.001
S5_DT_MAX = 0.1
BRANCH_WIDTH = 512
N_BRANCHES = 3
D_FF = 2816
N_EXPERTS = 8
TOP_K = 2
D_FF_EXPERT = 1408
N_DENSE = (DEPTH + 1) // 2
N_MOE = DEPTH // 2
IN_SIZES = (GLA_HEADS * GLA_DK, GLA_HEADS * GLA_DK, GLA_HEADS * GLA_DV, GLA_HEADS * GLA_DV, 2 * GLA_RANK,
            N_HEADS * HEAD_DIM, N_KV_HEADS * HEAD_DIM, N_KV_HEADS * HEAD_DIM, S5_WIDTH, N_BRANCHES * D_MODEL)
IN_SPLITS = tuple(int(s) for s in np.cumsum(IN_SIZES)[:-1])
IN_COLS = int(sum(IN_SIZES))

kernel_name = 'hybrid_gla_gqa_s5_prefix_diffusion_step'


def rms_norm(x, g):
    xf = x.astype(jnp.float32)
    y = xf * lax.rsqrt(jnp.mean(jnp.square(xf), axis=-1, keepdims=True) + EPS)
    return (y * g.astype(jnp.float32)).astype(x.dtype)


def grid_positions(n_tokens):
    rows = n_tokens // GRID_W
    t = jnp.arange(rows * GRID_W)
    return t // GRID_W, t % GRID_W


def axial_rope(x):
    row, col = grid_positions(x.shape[1])
    half = HEAD_DIM // 2
    inv_freq = 1.0 / (ROPE_THETA ** (jnp.arange(0, half, 2, dtype=jnp.float32) / half))

    def rotate(xa, pos):
        ang = pos.astype(jnp.float32)[:, None] * inv_freq[None, :]
        cos = jnp.cos(ang)[None, :, None, :]
        sin = jnp.sin(ang)[None, :, None, :]
        x1, x2 = jnp.split(xa.astype(jnp.float32), 2, axis=-1)
        return jnp.concatenate([x1 * cos - x2 * sin, x1 * sin + x2 * cos], axis=-1)

    out = jnp.concatenate([rotate(x[..., :half], row), rotate(x[..., half:], col)], axis=-1)
    return out.astype(x.dtype)


def block_attention(q, k, v):
    bsz, t_len = q.shape[:2]
    n_groups = N_HEADS // N_KV_HEADS
    nb = t_len // Q_BLOCK
    qb = q.reshape(bsz, nb, Q_BLOCK, N_KV_HEADS, n_groups, HEAD_DIM).transpose(1, 0, 2, 3, 4, 5)
    scale = HEAD_DIM ** -0.5

    def one_block(qblk):
        s = jnp.einsum('bqkgd,bskd->bkgqs', qblk, k, preferred_element_type=jnp.float32) * scale
        p = jax.nn.softmax(s, axis=-1).astype(v.dtype)
        return jnp.einsum('bkgqs,bskd->bqkgd', p, v)

    o = lax.map(one_block, qb)
    return o.transpose(1, 0, 2, 3, 4, 5).reshape(bsz, t_len, N_HEADS * HEAD_DIM)


def gla_scan(q, k, v, log_a, s0):
    bsz, t_len = q.shape[:2]
    n_chunks = t_len // GLA_CHUNK

    def chunks(a):
        a = a.astype(jnp.float32).reshape(bsz, n_chunks, GLA_CHUNK, GLA_HEADS, a.shape[-1])
        return a.transpose(1, 0, 3, 2, 4)

    lower = jnp.tril(jnp.ones((GLA_CHUNK, GLA_CHUNK), dtype=bool))[:, :, None]

    def step(S, inp):
        qc, kc, vc, ac = inp
        b = jnp.cumsum(ac, axis=2)
        o_inter = jnp.einsum('bhtd,bhde->bhte', qc * jnp.exp(b), S)
        diff = b[:, :, :, None, :] - b[:, :, None, :, :]
        decay = jnp.exp(jnp.where(lower, diff, -jnp.inf))
        scores = jnp.einsum('bhtd,bhsd,bhtsd->bhts', qc, kc, decay)
        o = o_inter + jnp.einsum('bhts,bhse->bhte', scores, vc)
        b_last = b[:, :, -1:, :]
        S_new = jnp.exp(b_last[:, :, 0, :, None]) * S + jnp.einsum('bhsd,bhse->bhde', kc * jnp.exp(b_last - b), vc)
        return S_new, o

    s_fin, o = lax.scan(step, s0.astype(jnp.float32), (chunks(q), chunks(k), chunks(v), chunks(log_a)))
    o = o.transpose(1, 0, 3, 2, 4).reshape(bsz, t_len, GLA_HEADS, GLA_DV)
    return o, s_fin


def gla_branch(q, k, v, g, a_low, wa2, ba, norm_g, s0):
    bsz, t_len = q.shape[:2]
    q = q.reshape(bsz, t_len, GLA_HEADS, GLA_DK) * (GLA_DK ** -0.5)
    k = k.reshape(bsz, t_len, GLA_HEADS, GLA_DK)
    v = v.reshape(bsz, t_len, GLA_HEADS, GLA_DV)
    z = jnp.einsum('btjr,jrk->btjk', a_low.reshape(bsz, t_len, 2, GLA_RANK), wa2) + ba
    log_a = (jax.nn.log_sigmoid(z.astype(jnp.float32)) / GLA_TAU).reshape(bsz, t_len, 2, GLA_HEADS, GLA_DK)
    rev = lambda a: jnp.flip(a, axis=1)
    o_f, s_f = gla_scan(q, k, v, log_a[:, :, 0], s0[:, 0])
    o_b, s_b = gla_scan(rev(q), rev(k), rev(v), rev(log_a[:, :, 1]), s0[:, 1])
    o = (o_f + rev(o_b)).astype(g.dtype)
    o = rms_norm(o, norm_g) * jax.nn.silu(g.reshape(bsz, t_len, GLA_HEADS, GLA_DV))
    return o.reshape(bsz, t_len, GLA_HEADS * GLA_DV), jnp.stack([s_f, s_b], axis=1)


def s5_direction(u, lam_re, lam_im, log_dt, b_re, b_im, c_re, c_im, x0):
    f32 = jnp.float32
    lam = lax.complex(lam_re.astype(f32), lam_im.astype(f32))
    dt = jnp.exp(log_dt.astype(f32))[:, None]
    a_bar = jnp.exp(lam * dt)
    b_bar = ((a_bar - 1.0) / lam)[:, :, None] * lax.complex(b_re.astype(f32), b_im.astype(f32))
    bu = jnp.einsum('gnp,btgp->btgn', b_bar, u.astype(jnp.complex64))
    bu = bu.at[:, 0].add(a_bar[None] * x0)
    a_seq = jnp.broadcast_to(a_bar, bu.shape)

    def combine(left, right):
        return right[0] * left[0], right[0] * left[1] + right[1]

    _, xs = lax.associative_scan(combine, (a_seq, bu), axis=1)
    y = jnp.einsum('gpn,btgn->btgp', lax.complex(c_re.astype(f32), c_im.astype(f32)), xs).real
    return y, xs[:, -1]


def s5_branch(u, p, x0):
    bsz, t_len, _ = u.shape
    uf = u.astype(jnp.float32).reshape(bsz, t_len, S5_GROUPS, S5_GROUP)
    rev = lambda a: jnp.flip(a, axis=1)
    y_f, x_f = s5_direction(uf, p['s5_lam_re'][0], p['s5_lam_im'][0], p['s5_log_dt'][0], p['s5_b_re'][0],
                            p['s5_b_im'][0], p['s5_c_re'][0], p['s5_c_im'][0], x0[:, 0])
    y_b, x_b = s5_direction(rev(uf), p['s5_lam_re'][1], p['s5_lam_im'][1], p['s5_log_dt'][1], p['s5_b_re'][1],
                            p['s5_b_im'][1], p['s5_c_re'][1], p['s5_c_im'][1], x0[:, 1])
    y = (y_f + rev(y_b)).reshape(bsz, t_len, S5_WIDTH) + p['s5_d'].astype(jnp.float32) * u.astype(jnp.float32)
    z = jax.nn.gelu(y).astype(u.dtype)
    out = z * jax.nn.sigmoid(z @ p['s5_glu_w'] + p['s5_glu_b'])
    return out, jnp.stack([x_f, x_b], axis=1)


def token_mixer(h, p, ctx):
    bsz, t_len, _ = h.shape
    proj = h @ p['w_in']
    gq, gk, gv, gg, ga, aq, ak, av, su, gate_in = jnp.split(proj, IN_SPLITS, axis=-1)
    q = rms_norm(aq.reshape(bsz, t_len, N_HEADS, HEAD_DIM), p['qk_g'][0])
    k = rms_norm(ak.reshape(bsz, t_len, N_KV_HEADS, HEAD_DIM), p['qk_g'][1])
    v = av.reshape(bsz, t_len, N_KV_HEADS, HEAD_DIM)
    if ctx is None:
        k_all, v_all = k, v
        gla_s0 = jnp.zeros((bsz, 2, GLA_HEADS, GLA_DK, GLA_DV), jnp.float32)
        s5_x0 = jnp.zeros((bsz, 2, S5_GROUPS, S5_STATE), jnp.complex64)
    else:
        k_ctx, v_ctx, gla_s0, s5_x0 = ctx
        q = axial_rope(q)
        k = axial_rope(k)
        k_all = jnp.concatenate([k_ctx.astype(k.dtype), k], axis=1)
        v_all = jnp.concatenate([v_ctx.astype(v.dtype), v], axis=1)
    att = block_attention(q, k_all, v_all)
    gla, gla_state = gla_branch(gq, gk, gv, gg, ga, p['gla_wa2'], p['gla_ba'], p['gla_norm_g'], gla_s0)
    s5, s5_state = s5_branch(su, p, s5_x0)
    branches = jnp.stack([gla, att, s5], axis=2)
    proj_b = jnp.einsum('btjw,jwd->btjd', branches, p['w_branch'])
    gates = jax.nn.sigmoid(gate_in.reshape(bsz, t_len, N_BRANCHES, D_MODEL))
    merged = jnp.sum(gates * proj_b, axis=2)
    return merged @ p['w_out'], (k, v, gla_state, s5_state)


def swiglu(h, w_gu, w_down):
    gt, up = jnp.split(h @ w_gu, 2, axis=-1)
    return (jax.nn.silu(gt) * up) @ w_down


def moe_ffn(h, router_w, router_b, w_gu, w_down):
    logits = (h @ router_w + router_b).astype(jnp.float32)
    top_val, top_idx = lax.top_k(logits, TOP_K)
    top_w = jax.nn.softmax(top_val, axis=-1)
    gate = jnp.einsum('btk,btke->bte', top_w, jax.nn.one_hot(top_idx, N_EXPERTS, dtype=jnp.float32)).astype(h.dtype)
    out = jnp.zeros_like(h)
    for e in range(N_EXPERTS):
        out = out + gate[..., e:e + 1] * swiglu(h, w_gu[e], w_down[e])
    return out


def trunk_layer(x, mod, p, ctx):
    shift_m, scale_m, gate_m, shift_f, scale_f, gate_f = [mod[:, None, i] for i in range(6)]
    h = rms_norm(x, p['norm_g'][0]) * (1.0 + scale_m) + shift_m
    m, ctx_out = token_mixer(h, p, ctx)
    x = x + gate_m * rms_norm(m, p['norm_g'][1])
    h = rms_norm(x, p['norm_g'][2]) * (1.0 + scale_f) + shift_f
    if 'router_w' in p:
        f = moe_ffn(h, p['router_w'], p['router_b'], p['moe_w_gu'], p['moe_w_down'])
    else:
        f = swiglu(h, p['ffn_w_gu'], p['ffn_w_down'])
    x = x + gate_f * rms_norm(f, p['norm_g'][3])
    return x, ctx_out


def setup_inputs(seed: int = 0) -> dict:
    key = jax.random.key(seed)
    k = jax.random.split(key, 40)
    f32 = jnp.float32

    def nrm(i, shape, scale):
        return scale * jax.random.normal(k[i], shape, f32)

    def gain(i, shape):
        return 1.0 + nrm(i, shape, 0.02)

    s5_shape = (DEPTH, 2, S5_GROUPS, S5_STATE)
    lam_im_init = jnp.pi * jnp.arange(S5_STATE, dtype=f32)
    return {
        'x_prompt': nrm(0, (BATCH, SEQ, D_MODEL), 1.0),
        'x_sample': nrm(1, (DEC_BATCH, DEC_SEQ, D_MODEL), 1.0),
        'cache_k': nrm(2, (DEC_BATCH, DEPTH, PAST_LEN, N_KV_HEADS, HEAD_DIM), 1.0),
        'cache_v': nrm(3, (DEC_BATCH, DEPTH, PAST_LEN, N_KV_HEADS, HEAD_DIM), 1.0),
        'state_gla': nrm(4, (DEC_BATCH, DEPTH, 2, GLA_HEADS, GLA_DK, GLA_DV), 0.5),
        'state_s5_re': nrm(5, (DEC_BATCH, DEPTH, 2, S5_GROUPS, S5_STATE), 0.1),
        'state_s5_im': nrm(6, (DEC_BATCH, DEPTH, 2, S5_GROUPS, S5_STATE), 0.1),
        'c': nrm(7, (DEC_BATCH, D_MODEL), 1.0),
        'c_ctx': nrm(8, (D_MODEL,), 1.0),
        'ada_w': nrm(9, (DEPTH, D_MODEL, 6 * D_MODEL), 0.5 * D_MODEL ** -0.5),
        'ada_b': nrm(10, (DEPTH, 6 * D_MODEL), 0.02),
        'norm_g': gain(11, (DEPTH, 4, D_MODEL)),
        'w_in': nrm(12, (DEPTH, D_MODEL, IN_COLS), D_MODEL ** -0.5),
        'gla_wa2': nrm(13, (DEPTH, 2, GLA_RANK, GLA_HEADS * GLA_DK), GLA_RANK ** -0.5),
        'gla_ba': nrm(14, (DEPTH, 2, GLA_HEADS * GLA_DK), 0.1),
        'gla_norm_g': gain(15, (DEPTH, GLA_DV)),
        'qk_g': gain(16, (DEPTH, 2, HEAD_DIM)),
        's5_lam_re': -0.5 + nrm(17, s5_shape, 0.01),
        's5_lam_im': lam_im_init + nrm(18, s5_shape, 0.01),
        's5_log_dt': jax.random.uniform(k[19], (DEPTH, 2, S5_GROUPS), f32, math.log(S5_DT_MIN), math.log(S5_DT_MAX)),
        's5_b_re': nrm(20, (DEPTH, 2, S5_GROUPS, S5_STATE, S5_GROUP), (2 * S5_GROUP) ** -0.5),
        's5_b_im': nrm(21, (DEPTH, 2, S5_GROUPS, S5_STATE, S5_GROUP), (2 * S5_GROUP) ** -0.5),
        's5_c_re': nrm(22, (DEPTH, 2, S5_GROUPS, S5_GROUP, S5_STATE), S5_STATE ** -0.5),
        's5_c_im': nrm(23, (DEPTH, 2, S5_GROUPS, S5_GROUP, S5_STATE), S5_STATE ** -0.5),
        's5_d': nrm(24, (DEPTH, S5_WIDTH), 1.0),
        's5_glu_w': nrm(25, (DEPTH, S5_WIDTH, S5_WIDTH), S5_WIDTH ** -0.5),
        's5_glu_b': nrm(26, (DEPTH, S5_WIDTH), 0.02),
        'w_branch': nrm(27, (DEPTH, N_BRANCHES, BRANCH_WIDTH, D_MODEL), BRANCH_WIDTH ** -0.5),
        'w_out': nrm(28, (DEPTH, D_MODEL, D_MODEL), D_MODEL ** -0.5),
        'ffn_w_gu': nrm(29, (N_DENSE, D_MODEL, 2 * D_FF), D_MODEL ** -0.5),
        'ffn_w_down': nrm(30, (N_DENSE, D_FF, D_MODEL), D_FF ** -0.5),
        'router_w': nrm(31, (N_MOE, D_MODEL, N_EXPERTS), D_MODEL ** -0.5),
        'router_b': nrm(32, (N_MOE, N_EXPERTS), 0.01),
        'moe_w_gu': nrm(33, (N_MOE, N_EXPERTS, D_MODEL, 2 * D_FF_EXPERT), D_MODEL ** -0.5),
        'moe_w_down': nrm(34, (N_MOE, N_EXPERTS, D_FF_EXPERT, D_MODEL), D_FF_EXPERT ** -0.5),
    }


def reference(x_prompt, x_sample, cache_k, cache_v, state_gla, state_s5_re, state_s5_im, c, c_ctx,
              ada_w, ada_b, norm_g, w_in, gla_wa2, gla_ba, gla_norm_g, qk_g,
              s5_lam_re, s5_lam_im, s5_log_dt, s5_b_re, s5_b_im, s5_c_re, s5_c_im, s5_d, s5_glu_w, s5_glu_b,
              w_branch, w_out, ffn_w_gu, ffn_w_down, router_w, router_b, moe_w_gu, moe_w_down):
    y_p = x_prompt
    y_s = x_sample
    ks, vs, glas, s5s = [], [], [], []
    for l in range(DEPTH):
        p = {
            'norm_g': norm_g[l], 'w_in': w_in[l], 'gla_wa2': gla_wa2[l], 'gla_ba': gla_ba[l],
            'gla_norm_g': gla_norm_g[l], 'qk_g': qk_g[l],
            's5_lam_re': s5_lam_re[l], 's5_lam_im': s5_lam_im[l], 's5_log_dt': s5_log_dt[l],
            's5_b_re': s5_b_re[l], 's5_b_im': s5_b_im[l], 's5_c_re': s5_c_re[l], 's5_c_im': s5_c_im[l],
            's5_d': s5_d[l], 's5_glu_w': s5_glu_w[l], 's5_glu_b': s5_glu_b[l],
            'w_branch': w_branch[l], 'w_out': w_out[l],
        }
        if l % 2 == 0:
            p['ffn_w_gu'] = ffn_w_gu[l // 2]
            p['ffn_w_down'] = ffn_w_down[l // 2]
        else:
            p['router_w'] = router_w[l // 2]
            p['router_b'] = router_b[l // 2]
            p['moe_w_gu'] = moe_w_gu[l // 2]
            p['moe_w_down'] = moe_w_down[l // 2]
        mod_ctx = (jax.nn.silu(c_ctx[None]) @ ada_w[l] + ada_b[l]).reshape(1, 6, D_MODEL)
        mod_lat = (jax.nn.silu(c) @ ada_w[l] + ada_b[l]).reshape(-1, 6, D_MODEL)
        y_p, (k_l, v_l, gla_l, s5_l) = trunk_layer(y_p, mod_ctx, p, None)
        ks.append(k_l)
        vs.append(v_l)
        glas.append(gla_l)
        s5s.append(s5_l)
        s5_x0 = lax.complex(state_s5_re[:, l].astype(jnp.float32), state_s5_im[:, l].astype(jnp.float32))
        ctx = (cache_k[:, l], cache_v[:, l], state_gla[:, l], s5_x0)
        y_s, _ = trunk_layer(y_s, mod_lat, p, ctx)
    new_cache_k = jnp.stack(ks, axis=1)
    new_cache_v = jnp.stack(vs, axis=1)
    new_state_gla = jnp.stack(glas, axis=1)
    new_s5 = jnp.stack(s5s, axis=1)
    return (y_p, y_s, new_cache_k, new_cache_v, new_state_gla, new_s5.real, new_s5.imag)
```

```python
import functools

import jax
import jax.numpy as jnp
import numpy as np
from jax import lax
from jax.experimental import pallas as pl
from jax.experimental.pallas import tpu as pltpu

F32 = jnp.float32
BF16 = jnp.bfloat16

D_MODEL = 1024
GRID_W = 64
EPS = 1e-6
GLA_HEADS = 4
GLA_DK = 64
GLA_DV = 128
GLA_RANK = 16
GLA_TAU = 16.0
GLA_CHUNK = 64
N_HEADS = 8
N_KV_HEADS = 2
HEAD_DIM = 64
ROPE_THETA = 10000.0
S5_WIDTH = 512
S5_GROUP = 16
S5_GROUPS = S5_WIDTH // S5_GROUP
S5_STATE = 64
S5_CHUNK = 16
BRANCH_WIDTH = 512
N_BRANCHES = 3
D_FF = 2816
N_EXPERTS = 8
D_FF_EXPERT = 1408

_IN_NAMES = ("gq", "gk", "gv", "gg", "ga", "aq", "ak", "av", "su", "gate")
_IN_SIZES = (256, 256, 512, 512, 32, 512, 128, 128, 512, 3072)
_P_ORDER = ("gate", "gq", "gk", "gv", "gg", "aq", "su", "ak", "av", "ga")
P_COLS = 6144
LANES = 128
ROW_TILE = 256
VMEM_LIMIT = 56 * 1024 * 1024


def _p_layout():
    off, out = 0, {}
    sizes = dict(zip(_IN_NAMES, _IN_SIZES))
    for n in _P_ORDER:
        width = max(sizes[n], LANES) if n == "ga" else sizes[n]
        blk = 1024 if n == "gate" else width
        assert off % blk == 0
        out[n] = (off, blk, off // blk)
        off += width
    assert off <= P_COLS
    return out


_P = _p_layout()


def _sigmoid(x):
    return 1.0 / (1.0 + jnp.exp(-x))


def _silu(x):
    return x * _sigmoid(x)


def _rms(x, g):
    return x * lax.rsqrt(jnp.mean(x * x, axis=-1, keepdims=True) + EPS) * g


def _dot(a, b):
    return jnp.dot(a, b, preferred_element_type=F32)


def _dot_nt(a, b):
    return lax.dot_general(a, b, (((1,), (1,)), ((), ())), preferred_element_type=F32)


def _dot_tn(a, b):
    return lax.dot_general(a, b, (((0,), (0,)), ((), ())), preferred_element_type=F32)


def _split3(x):
    h1 = x.astype(BF16)
    r1 = x - h1.astype(F32)
    h2 = r1.astype(BF16)
    h3 = (r1 - h2.astype(F32)).astype(BF16)
    return h1, h2, h3


def _params(sem, vmem=VMEM_LIMIT):
    return pltpu.CompilerParams(dimension_semantics=sem, vmem_limit_bytes=vmem)


def _const_spec(shape):
    nd = len(shape)
    return pl.BlockSpec(shape, lambda *_: (0,) * nd)


def _mod_kernel(c_ref, w_ref, b_ref, o_ref):
    c = c_ref[...]
    o_ref[...] = _dot(_silu(c).astype(BF16), w_ref[...].astype(BF16)) + b_ref[...]


def _modulation(cvec, ada_w, ada_b):
    depth = ada_w.shape[0]
    nct = 6
    out = pl.pallas_call(
        _mod_kernel,
        grid=(depth, nct),
        in_specs=[
            pl.BlockSpec((8, D_MODEL), lambda l, j: (0, 0)),
            pl.BlockSpec((None, D_MODEL, D_MODEL), lambda l, j: (l, 0, j)),
            pl.BlockSpec((None, 1, D_MODEL), lambda l, j: (l, 0, j)),
        ],
        out_specs=pl.BlockSpec((None, 8, D_MODEL), lambda l, j: (l, 0, j)),
        out_shape=jax.ShapeDtypeStruct((depth, 8, 6 * D_MODEL), F32),
        compiler_params=_params(("arbitrary", "arbitrary")),
    )(cvec, ada_w, ada_b.reshape(depth, 1, 6 * D_MODEL))
    return out.reshape(depth, 8, 6, D_MODEL)


def _mod_spec(tm, ctx_rows, lat_len):
    def idx(i, *_):
        start = i * tm
        return (jnp.where(start < ctx_rows, 0, 1 + (start - ctx_rows) // lat_len), 0, 0)
    return pl.BlockSpec((None, 6, D_MODEL), idx)


def _inproj_kernel(x_ref, mod_ref, g_ref, w_ref, o_ref, *, tn):
    h = _rms(x_ref[...], g_ref[...]) * (1.0 + mod_ref[1:2, :]) + mod_ref[0:1, :]
    hb = h.astype(BF16)
    for j in range(P_COLS // tn):
        o_ref[:, j * tn:(j + 1) * tn] = _dot(hb, w_ref[:, j * tn:(j + 1) * tn])


def _inproj(x, mod, g, w, ctx_rows, lat_len):
    nt = x.shape[0]
    tm = ROW_TILE
    return pl.pallas_call(
        functools.partial(_inproj_kernel, tn=1024),
        grid=(nt // tm,),
        in_specs=[
            pl.BlockSpec((tm, D_MODEL), lambda i: (i, 0)),
            _mod_spec(tm, ctx_rows, lat_len),
            _const_spec((1, D_MODEL)),
            _const_spec((D_MODEL, P_COLS)),
        ],
        out_specs=pl.BlockSpec((tm, P_COLS), lambda i: (i, 0)),
        out_shape=jax.ShapeDtypeStruct((nt, P_COLS), F32),
        compiler_params=_params(("parallel",)),
    )(x, mod, g, w)


def _group_mean_sq(x, bd, width):
    xx = x * x
    hi = xx.astype(BF16)
    lo = (xx - hi.astype(F32)).astype(BF16)
    return (_dot(hi, bd) + _dot(lo, bd)) * (1.0 / width)


def _rope(x, cos, sa, sb):
    n = x.shape[-1]
    return x * cos + pltpu.roll(x, n - 16, 1) * sa + pltpu.roll(x, 16, 1) * sb


def _qkprep_kernel(*refs, rope):
    if rope:
        q_ref, k_ref, qg_ref, kg_ref, bd_ref, cos_ref, sa_ref, sb_ref, qo_ref, ko_ref = refs
    else:
        q_ref, k_ref, qg_ref, kg_ref, bd_ref, qo_ref, ko_ref, kf_ref = refs
    kw = N_KV_HEADS * HEAD_DIM
    q = q_ref[...]
    k = k_ref[...]
    q = q * lax.rsqrt(_group_mean_sq(q, bd_ref[...], HEAD_DIM) + EPS) * qg_ref[...]
    k = k * lax.rsqrt(_group_mean_sq(k, bd_ref[0:kw, 0:kw], HEAD_DIM) + EPS) * kg_ref[...]
    if rope:
        q = _rope(q, cos_ref[...], sa_ref[...], sb_ref[...])
        k = _rope(k, cos_ref[:, 0:kw], sa_ref[:, 0:kw], sb_ref[:, 0:kw])
    else:
        kf_ref[...] = k
    qo_ref[...] = (q * (HEAD_DIM ** -0.5)).astype(BF16)
    ko_ref[...] = k.astype(BF16)


def _qkprep(p, row0, rows, qg, kg, bd, rope_tabs):
    tm = ROW_TILE
    t0 = row0 // tm
    qw, kw = N_HEADS * HEAD_DIM, N_KV_HEADS * HEAD_DIM
    aq, ak = _P["aq"][2], _P["ak"][2]
    in_specs = [
        pl.BlockSpec((tm, qw), lambda i: (t0 + i, aq)),
        pl.BlockSpec((tm, kw), lambda i: (t0 + i, ak)),
        _const_spec((1, qw)),
        _const_spec((1, kw)),
        _const_spec((qw, qw)),
    ]
    args = [p, p, qg, kg, bd]
    out_specs = [pl.BlockSpec((tm, qw), lambda i: (i, 0)), pl.BlockSpec((tm, kw), lambda i: (i, 0))]
    out_shape = [jax.ShapeDtypeStruct((rows, qw), BF16), jax.ShapeDtypeStruct((rows, kw), BF16)]
    if rope_tabs is not None:
        nper = rope_tabs[0].shape[0] // tm
        in_specs += [pl.BlockSpec((tm, qw), lambda i: (i % nper, 0))] * 3
        args += list(rope_tabs)
    else:
        out_specs.append(pl.BlockSpec((tm, kw), lambda i: (i, 0)))
        out_shape.append(jax.ShapeDtypeStruct((rows, kw), F32))
    return pl.pallas_call(
        functools.partial(_qkprep_kernel, rope=rope_tabs is not None),
        grid=(rows // tm,),
        in_specs=in_specs,
        out_specs=out_specs,
        out_shape=out_shape,
        compiler_params=_params(("parallel",)),
    )(*args)


def _rope_tables(t_len):
    half = HEAD_DIM // 2
    t = jnp.arange(t_len)
    row, col = (t // GRID_W).astype(F32), (t % GRID_W).astype(F32)
    inv_freq = 1.0 / (ROPE_THETA ** (jnp.arange(0, half, 2, dtype=F32) / half))
    ar, ac = row[:, None] * inv_freq[None, :], col[:, None] * inv_freq[None, :]
    zero = jnp.zeros_like(ar)
    cos = jnp.concatenate([jnp.cos(ar), jnp.cos(ar), jnp.cos(ac), jnp.cos(ac)], axis=-1)
    sa = jnp.concatenate([-jnp.sin(ar), zero, -jnp.sin(ac), zero], axis=-1)
    sb = jnp.concatenate([zero, jnp.sin(ar), zero, jnp.sin(ac)], axis=-1)
    return tuple(jnp.tile(a, (1, N_HEADS)) for a in (cos, sa, sb))


def _attn_kernel(*refs, n_src):
    q_ref = refs[0]
    k_refs = refs[1:1 + n_src]
    v_refs = refs[1 + n_src:1 + 2 * n_src]
    o_ref = refs[1 + 2 * n_src]
    groups = N_HEADS // N_KV_HEADS
    for kv in range(N_KV_HEADS):
        sl = slice(kv * HEAD_DIM, (kv + 1) * HEAD_DIM)
        ks = [r[:, sl].astype(BF16) for r in k_refs]
        vs = [r[:, sl].astype(BF16) for r in v_refs]
        for g in range(groups):
            h = kv * groups + g
            qh = q_ref[:, h * HEAD_DIM:(h + 1) * HEAD_DIM]
            ss = [_dot_nt(qh, k) for k in ks]
            m = ss[0].max(axis=-1, keepdims=True)
            for s in ss[1:]:
                m = jnp.maximum(m, s.max(axis=-1, keepdims=True))
            ps = [jnp.exp(s - m) for s in ss]
            den = ps[0].sum(axis=-1, keepdims=True)
            for p in ps[1:]:
                den = den + p.sum(axis=-1, keepdims=True)
            o = _dot(ps[0].astype(BF16), vs[0])
            for p, v in zip(ps[1:], vs[1:]):
                o = o + _dot(p.astype(BF16), v)
            o_ref[:, h * HEAD_DIM:(h + 1) * HEAD_DIM] = (o / den).astype(o_ref.dtype)


def _attention_ctx(qn, kn, p, n_seq, t_len):
    qw, kw = N_HEADS * HEAD_DIM, N_KV_HEADS * HEAD_DIM
    av = _P["av"][2]
    return pl.pallas_call(
        functools.partial(_attn_kernel, n_src=1),
        grid=(n_seq,),
        in_specs=[
            pl.BlockSpec((t_len, qw), lambda b: (b, 0)),
            pl.BlockSpec((t_len, kw), lambda b: (b, 0)),
            pl.BlockSpec((t_len, kw), lambda b: (b, av)),
        ],
        out_specs=pl.BlockSpec((t_len, qw), lambda b: (b, 0)),
        out_shape=jax.ShapeDtypeStruct((n_seq * t_len, qw), BF16),
        compiler_params=_params(("parallel",)),
    )(qn, kn, p)


def _attention_lat(qn, kn, p, cache_k, cache_v, n_seq, t_len, seq_blk0, tq=256):
    qw, kw = N_HEADS * HEAD_DIM, N_KV_HEADS * HEAD_DIM
    av = _P["av"][2]
    past = cache_k.shape[1]
    nq = t_len // tq
    return pl.pallas_call(
        functools.partial(_attn_kernel, n_src=2),
        grid=(n_seq, nq),
        in_specs=[
            pl.BlockSpec((tq, qw), lambda b, i: (b * nq + i, 0)),
            pl.BlockSpec((None, past, kw), lambda b, i: (b, 0, 0)),
            pl.BlockSpec((t_len, kw), lambda b, i: (b, 0)),
            pl.BlockSpec((None, past, kw), lambda b, i: (b, 0, 0)),
            pl.BlockSpec((t_len, kw), lambda b, i: (seq_blk0 + b, av)),
        ],
        out_specs=pl.BlockSpec((tq, qw), lambda b, i: (b * nq + i, 0)),
        out_shape=jax.ShapeDtypeStruct((n_seq * t_len, qw), BF16),
        compiler_params=_params(("parallel", "arbitrary")),
    )(qn, cache_k, kn, cache_v, p)


def _gla_kernel(*refs, t_len, has_init, want_state):
    q_ref, k_ref, v_ref, a_ref, wa_ref, ba_ref = refs[:6]
    pos = 6
    s0_ref = None
    if has_init:
        s0_ref = refs[pos]
        pos += 1
    o_ref = refs[pos]
    pos += 1
    so_ref = None
    if want_state:
        so_ref = refs[pos]
        pos += 1
    s_scr = refs[pos]

    L = GLA_CHUNK
    n_chunks = t_len // L
    row = lax.broadcasted_iota(jnp.int32, (L, L), 0)
    col = lax.broadcasted_iota(jnp.int32, (L, L), 1)
    eye = (row == col).astype(F32)

    for d in range(2):
        keep = (col <= row) if d == 0 else (col >= row)
        tri = jnp.where(keep, 1.0, 0.0).astype(BF16)
        last = L - 1 if d == 0 else 0
        if has_init:
            s_scr[...] = s0_ref[d]
        else:
            s_scr[...] = jnp.zeros(s_scr.shape, F32)

        def body(i, carry, d=d, keep=keep, tri=tri, last=last):
            c = i if d == 0 else n_chunks - 1 - i
            rows = pl.ds(pl.multiple_of(c * L, L), L)
            a_low = a_ref[rows, :][:, d * GLA_RANK:(d + 1) * GLA_RANK]
            z = _dot(a_low.astype(BF16), wa_ref[d]) + ba_ref[d]
            log_a = (jnp.minimum(z, 0.0) - jnp.log(1.0 + jnp.exp(-jnp.abs(z)))) * (1.0 / GLA_TAU)
            l1, l2, l3 = _split3(log_a)
            b = _dot(tri, l1) + _dot(tri, l2) + _dot(tri, l3)
            b_last = b[last:last + 1, :]
            q = q_ref[rows, :] * (GLA_DK ** -0.5)
            qe = q * jnp.exp(b)
            qt = (qe * jnp.exp(-b_last)).astype(BF16)
            kt = (k_ref[rows, :] * jnp.exp(b_last - b)).astype(BF16)
            qe = qe.astype(BF16)
            e_last = jnp.exp(b_last)
            for h in range(GLA_HEADS):
                ks = slice(h * GLA_DK, (h + 1) * GLA_DK)
                vs = slice(h * GLA_DV, (h + 1) * GLA_DV)
                vh = v_ref[rows, vs].astype(BF16)
                s_h = s_scr[h]
                scores = jnp.where(keep, _dot_nt(qt[:, ks], kt[:, ks]), 0.0)
                o = _dot(qe[:, ks], s_h.astype(BF16)) + _dot(scores.astype(BF16), vh)
                decay_col = jnp.sum(eye * e_last[:, ks], axis=1, keepdims=True)
                s_scr[h] = decay_col * s_h + _dot_tn(kt[:, ks], vh)
                if d == 0:
                    o_ref[rows, vs] = o
                else:
                    o_ref[rows, vs] = o_ref[rows, vs] + o
            return carry

        lax.fori_loop(0, n_chunks, body, 0)
        if want_state:
            so_ref[d] = s_scr[...]


def _gla(p, wa2, ba, s0, n_seq, t_len, seq_blk0, want_state):
    kq, kk, kv, ka = _P["gq"][2], _P["gk"][2], _P["gv"][2], _P["ga"][2]
    hk, hv = GLA_HEADS * GLA_DK, GLA_HEADS * GLA_DV
    in_specs = [
        pl.BlockSpec((t_len, hk), lambda b: (seq_blk0 + b, kq)),
        pl.BlockSpec((t_len, hk), lambda b: (seq_blk0 + b, kk)),
        pl.BlockSpec((t_len, hv), lambda b: (seq_blk0 + b, kv)),
        pl.BlockSpec((t_len, LANES), lambda b: (seq_blk0 + b, ka)),
        _const_spec((2, GLA_RANK, hk)),
        _const_spec((2, 1, hk)),
    ]
    args = [p, p, p, p, wa2, ba]
    state_shape = (2, GLA_HEADS, GLA_DK, GLA_DV)
    state_spec = pl.BlockSpec((None,) + state_shape, lambda b: (b, 0, 0, 0, 0))
    if s0 is not None:
        in_specs.append(state_spec)
        args.append(s0)
    out_specs = [pl.BlockSpec((t_len, hv), lambda b: (b, 0))]
    out_shape = [jax.ShapeDtypeStruct((n_seq * t_len, hv), F32)]
    if want_state:
        out_specs.append(state_spec)
        out_shape.append(jax.ShapeDtypeStruct((n_seq,) + state_shape, F32))
    return pl.pallas_call(
        functools.partial(_gla_kernel, t_len=t_len, has_init=s0 is not None, want_state=want_state),
        grid=(n_seq,),
        in_specs=in_specs,
        out_specs=out_specs,
        out_shape=out_shape,
        scratch_shapes=[pltpu.VMEM((GLA_HEADS, GLA_DK, GLA_DV), F32)],
        compiler_params=_params(("parallel",)),
    )(*args)


def _s5_matrices(lam_re, lam_im, log_dt, b_re, b_im, c_re, c_im):
    hi = lax.Precision.HIGHEST
    lc = S5_CHUNK
    lam = lax.complex(lam_re.astype(F32), lam_im.astype(F32))
    dt = jnp.exp(log_dt.astype(F32))[..., None]
    a_bar = jnp.exp(lam * dt)
    b_bar = ((a_bar - 1.0) / lam)[..., None] * lax.complex(b_re.astype(F32), b_im.astype(F32))
    cc = lax.complex(c_re.astype(F32), c_im.astype(F32))
    taus = jnp.arange(lc + 1, dtype=F32)
    apow = jnp.exp((lam * dt)[None] * taus[:, None, None, None].astype(jnp.complex64))
    resp = jnp.einsum("dgpn,tdgn,dgnq->dgtpq", cc, apow[:lc], b_bar, precision=hi).real
    s_idx = jnp.arange(lc)[:, None]
    t_idx = jnp.arange(lc)[None, :]

    def toeplitz(r, lag, keep):
        m = r[:, jnp.clip(lag, 0, lc - 1)]
        m = jnp.where(keep[None, :, :, None, None], m, 0.0)
        return m.transpose(0, 1, 4, 2, 3)

    m_f = toeplitz(resp[0], t_idx - s_idx, s_idx <= t_idx)
    m_b = toeplitz(resp[1], s_idx - t_idx, s_idx >= t_idx)
    g = lam.shape[1]
    m_sum = (m_f + m_b).reshape(g, lc * S5_GROUP, lc * S5_GROUP)

    def increments(ap, bb):
        w = ap.transpose(1, 0, 2)[:, :, None, :] * bb.transpose(0, 2, 1)[:, None, :, :]
        w = w.reshape(g, lc * S5_GROUP, S5_STATE)
        return jnp.concatenate([w.real, w.imag, w.imag, w.real], axis=-1)

    w_f = increments(apow[:lc, 0][::-1], b_bar[0])
    w_b = increments(apow[:lc, 1], b_bar[1])
    mw = jnp.concatenate([m_sum, w_f, w_b], axis=-1).astype(BF16)

    def state_response(ap, c):
        ca = c[:, None, :, :] * ap.transpose(1, 0, 2)[:, :, None, :]
        ca = ca.reshape(g, lc * S5_GROUP, S5_STATE).transpose(0, 2, 1)
        return jnp.concatenate([ca.real, -ca.imag], axis=1)

    v_f = state_response(apow[1:lc + 1, 0], cc[0])
    v_b = state_response(apow[1:lc + 1, 1][::-1], cc[1])
    v = jnp.concatenate([v_f, v_b], axis=1).astype(BF16)
    a_chunk = apow[lc]
    a_r = jnp.concatenate([a_chunk.real, a_chunk.real], axis=-1)
    a_i = jnp.concatenate([-a_chunk.imag, a_chunk.imag], axis=-1)
    return mw, v, a_r, a_i


def _s5_local_kernel(u_ref, mw_ref, yi_ref, dx_ref):
    w = S5_CHUNK * S5_GROUP
    r = _dot(u_ref[...], mw_ref[...])
    yi_ref[...] = r[:, :w]
    dx_ref[...] = r[:, w:]


def _s5_local(u, mw):
    g, nc, w = u.shape
    return pl.pallas_call(
        _s5_local_kernel,
        grid=(g,),
        in_specs=[pl.BlockSpec((None, nc, w), lambda i: (i, 0, 0)),
                  pl.BlockSpec((None, w, 3 * w), lambda i: (i, 0, 0))],
        out_specs=[pl.BlockSpec((None, nc, w), lambda i: (i, 0, 0)),
                   pl.BlockSpec((None, nc, 2 * w), lambda i: (i, 0, 0))],
        out_shape=[jax.ShapeDtypeStruct((g, nc, w), F32), jax.ShapeDtypeStruct((g, nc, 2 * w), F32)],
        compiler_params=_params(("parallel",)),
    )(u, mw)


def _s5_scan_kernel(*refs, nc, has_init):
    if has_init:
        dx_ref, ar_ref, ai_ref, x0_ref, xp_ref, xo_ref = refs
    else:
        dx_ref, ar_ref, ai_ref, xp_ref, xo_ref = refs
    n2 = 2 * S5_STATE
    arf, aif, arb, aib = ar_ref[0], ai_ref[0], ar_ref[1], ai_ref[1]
    if has_init:
        init = (x0_ref[0, 0], x0_ref[0, 1], x0_ref[1, 0], x0_ref[1, 1])
    else:
        zero = jnp.zeros((S5_GROUPS, n2), F32)
        init = (zero, zero, zero, zero)

    def body(i, carry):
        xf, xsf, xb, xsb = carry
        cb = nc - 1 - i
        xp_ref[i, :, 0:n2] = xf
        xp_ref[cb, :, n2:2 * n2] = xb
        df = dx_ref[i]
        db = dx_ref[cb]
        nxf = arf * xf + aif * xsf + df[:, 0:n2]
        nxsf = arf * xsf - aif * xf + df[:, n2:2 * n2]
        nxb = arb * xb + aib * xsb + db[:, 2 * n2:3 * n2]
        nxsb = arb * xsb - aib * xb + db[:, 3 * n2:4 * n2]
        return nxf, nxsf, nxb, nxsb

    xf, _, xb, _ = lax.fori_loop(0, nc, body, init)
    xo_ref[0] = xf
    xo_ref[1] = xb


def _s5_scan(dx, a_r, a_i, x0, n_seq, nc, blk0):
    g, n2 = S5_GROUPS, 2 * S5_STATE
    in_specs = [
        pl.BlockSpec((nc, g, 4 * n2), lambda b: (blk0 + b, 0, 0)),
        _const_spec((2, g, n2)),
        _const_spec((2, g, n2)),
    ]
    args = [dx, a_r, a_i]
    if x0 is not None:
        in_specs.append(pl.BlockSpec((None, 2, 2, g, n2), lambda b: (b, 0, 0, 0, 0)))
        args.append(x0)
    return pl.pallas_call(
        functools.partial(_s5_scan_kernel, nc=nc, has_init=x0 is not None),
        grid=(n_seq,),
        in_specs=in_specs,
        out_specs=[pl.BlockSpec((nc, g, 2 * n2), lambda b: (b, 0, 0)),
                   pl.BlockSpec((None, 2, g, n2), lambda b: (b, 0, 0, 0))],
        out_shape=[jax.ShapeDtypeStruct((n_seq * nc, g, 2 * n2), F32),
                   jax.ShapeDtypeStruct((n_seq, 2, g, n2), F32)],
        compiler_params=_params(("parallel",)),
    )(*args)


def _s5_out_kernel(xp_ref, v_ref, yi_ref, y_ref):
    y_ref[...] = yi_ref[...] + _dot(xp_ref[...].astype(BF16), v_ref[...])


def _s5_out(xp, v, yi):
    g, nc, w = yi.shape
    return pl.pallas_call(
        _s5_out_kernel,
        grid=(g,),
        in_specs=[pl.BlockSpec((None, nc, w), lambda i: (i, 0, 0)),
                  pl.BlockSpec((None, w, w), lambda i: (i, 0, 0)),
                  pl.BlockSpec((None, nc, w), lambda i: (i, 0, 0))],
        out_specs=pl.BlockSpec((None, nc, w), lambda i: (i, 0, 0)),
        out_shape=jax.ShapeDtypeStruct((g, nc, w), F32),
        compiler_params=_params(("parallel",)),
    )(xp, v, yi)


def _s5(p, mats, x0_lat, ctx_seqs, ctx_len, lat_seqs, lat_len):
    mw, v, a_r, a_i = mats
    nt = p.shape[0]
    lc, g, pw = S5_CHUNK, S5_GROUPS, S5_GROUP
    off = _P["su"][0]
    u = p[:, off:off + S5_WIDTH].astype(BF16)
    u = u.reshape(nt // lc, lc, g, pw).transpose(2, 0, 1, 3).reshape(g, nt // lc, lc * pw)
    yi, dx = _s5_local(u, mw)
    dx = dx.transpose(1, 0, 2)
    ncc, ncl = ctx_len // lc, lat_len // lc
    xp_c, xfin = _s5_scan(dx, a_r, a_i, None, ctx_seqs, ncc, 0)
    xp_l, _ = _s5_scan(dx, a_r, a_i, x0_lat, lat_seqs, ncl, ctx_seqs * ncc // ncl)
    xp = jnp.concatenate([xp_c, xp_l], axis=0).transpose(1, 0, 2)
    y = _s5_out(xp, v, yi)
    y = y.reshape(g, nt // lc, lc, pw).transpose(1, 2, 0, 3).reshape(nt, S5_WIDTH)
    return y, xfin


def _gelu_tanh(x):
    return 0.5 * x * (1.0 + jnp.tanh(np.sqrt(2.0 / np.pi).astype(np.float32) * (x + 0.044715 * (x * x * x))))


def _merge_kernel(x_ref, mod_ref, g0_ref, g1_ref, g2_ref, gg_ref, gla_ref, att_ref, s5y_ref, su_ref,
                  gng_ref, s5d_ref, gluw_ref, glub_ref, wb_ref, wo_ref, g_ref, o_ref):
    parts = []
    for h in range(GLA_HEADS):
        sl = slice(h * GLA_DV, (h + 1) * GLA_DV)
        parts.append(_rms(gla_ref[:, sl], gng_ref[...]) * _silu(gg_ref[:, sl]))
    gla = jnp.concatenate(parts, axis=-1).astype(BF16)
    z = _gelu_tanh(s5y_ref[...] + s5d_ref[...] * su_ref[...])
    zb = z.astype(BF16)
    s5 = (z * _sigmoid(_dot(zb, gluw_ref[...]) + glub_ref[...])).astype(BF16)
    merged = _sigmoid(g0_ref[...]) * _dot(gla, wb_ref[0])
    merged = merged + _sigmoid(g1_ref[...]) * _dot(att_ref[...], wb_ref[1])
    merged = merged + _sigmoid(g2_ref[...]) * _dot(s5, wb_ref[2])
    m = _dot(merged.astype(BF16), wo_ref[...])
    o_ref[...] = x_ref[...] + mod_ref[2:3, :] * _rms(m, g_ref[...])


def _merge(x, mod, p, gla_o, att, s5y, gng, s5d, gluw, glub, wb, wo, g, ctx_rows, lat_len):
    nt = x.shape[0]
    tm = ROW_TILE
    bw = BRANCH_WIDTH
    row = lambda w, j: pl.BlockSpec((tm, w), lambda i: (i, j))
    in_specs = [
        row(D_MODEL, 0),
        _mod_spec(tm, ctx_rows, lat_len),
        row(D_MODEL, 0), row(D_MODEL, 1), row(D_MODEL, 2),
        row(bw, _P["gg"][2]),
        row(bw, 0), row(bw, 0), row(bw, 0),
        row(bw, _P["su"][2]),
        _const_spec((1, GLA_DV)), _const_spec((1, bw)), _const_spec((bw, bw)), _const_spec((1, bw)),
        _const_spec((N_BRANCHES, bw, D_MODEL)), _const_spec((D_MODEL, D_MODEL)), _const_spec((1, D_MODEL)),
    ]
    return pl.pallas_call(
        _merge_kernel,
        grid=(nt // tm,),
        in_specs=in_specs,
        out_specs=row(D_MODEL, 0),
        out_shape=jax.ShapeDtypeStruct((nt, D_MODEL), F32),
        compiler_params=_params(("parallel",)),
    )(x, mod, p, p, p, p, gla_o, att, s5y, p, gng, s5d, gluw, glub, wb, wo, g)


def _ffn_kernel(x_ref, mod_ref, g2_ref, g3_ref, wgu_ref, wd_ref, o_ref, *, fc):
    x = x_ref[...]
    h = (_rms(x, g2_ref[...]) * (1.0 + mod_ref[4:5, :]) + mod_ref[3:4, :]).astype(BF16)
    acc = jnp.zeros(x.shape, F32)
    for j in range(D_FF // fc):
        gt = _dot(h, wgu_ref[:, j * fc:(j + 1) * fc])
        up = _dot(h, wgu_ref[:, D_FF + j * fc:D_FF + (j + 1) * fc])
        acc = acc + _dot((_silu(gt) * up).astype(BF16), wd_ref[j * fc:(j + 1) * fc, :])
    o_ref[...] = x + mod_ref[5:6, :] * _rms(acc, g3_ref[...])


def _ffn(x, mod, g2, g3, wgu, wd, ctx_rows, lat_len):
    nt = x.shape[0]
    tm = ROW_TILE
    return pl.pallas_call(
        functools.partial(_ffn_kernel, fc=D_FF // 2),
        grid=(nt // tm,),
        in_specs=[
            pl.BlockSpec((tm, D_MODEL), lambda i: (i, 0)),
            _mod_spec(tm, ctx_rows, lat_len),
            _const_spec((1, D_MODEL)), _const_spec((1, D_MODEL)),
            _const_spec((D_MODEL, 2 * D_FF)), _const_spec((D_FF, D_MODEL)),
        ],
        out_specs=pl.BlockSpec((tm, D_MODEL), lambda i: (i, 0)),
        out_shape=jax.ShapeDtypeStruct((nt, D_MODEL), F32),
        compiler_params=_params(("parallel",)),
    )(x, mod, g2, g3, wgu, wd)


def _moe_kernel(x_ref, mod_ref, g2_ref, g3_ref, rw_ref, rb_ref, wgu_ref, wd_ref, o_ref, h_scr, gate_scr, acc_scr):
    e = pl.program_id(1)
    lane = lax.broadcasted_iota(jnp.int32, gate_scr.shape, 1)

    @pl.when(e == 0)
    def _():
        h = _rms(x_ref[...], g2_ref[...]) * (1.0 + mod_ref[4:5, :]) + mod_ref[3:4, :]
        h_scr[...] = h.astype(BF16)
        h1, h2, h3 = _split3(h)
        w1, w2, w3 = rw_ref[0], rw_ref[1], rw_ref[2]
        logits = (_dot(h2, w2) + _dot(h1, w3) + _dot(h3, w1)) + (_dot(h1, w2) + _dot(h2, w1)) + _dot(h1, w1)
        logits = jnp.where(lane < N_EXPERTS, logits + rb_ref[...], -jnp.inf)
        m1 = logits.max(axis=-1, keepdims=True)
        i1 = jnp.where(logits == m1, lane, LANES).min(axis=-1, keepdims=True)
        rest = jnp.where(lane == i1, -jnp.inf, logits)
        m2 = rest.max(axis=-1, keepdims=True)
        i2 = jnp.where(rest == m2, lane, LANES).min(axis=-1, keepdims=True)
        e2 = jnp.exp(m2 - m1)
        w_top = 1.0 / (1.0 + e2)
        gate_scr[...] = jnp.where(lane == i1, w_top, 0.0) + jnp.where(lane == i2, e2 * w_top, 0.0)
        acc_scr[...] = jnp.zeros(acc_scr.shape, F32)

    h = h_scr[...]
    gt = _dot(h, wgu_ref[:, :D_FF_EXPERT])
    up = _dot(h, wgu_ref[:, D_FF_EXPERT:])
    y = _dot((_silu(gt) * up).astype(BF16), wd_ref[...])
    g_e = jnp.sum(jnp.where(lane == e, gate_scr[...], 0.0), axis=-1, keepdims=True)
    acc_scr[...] += g_e * y

    @pl.when(e == N_EXPERTS - 1)
    def _():
        o_ref[...] = x_ref[...] + mod_ref[5:6, :] * _rms(acc_scr[...], g3_ref[...])


def _moe(x, mod, g2, g3, rw, rb, wgu, wd, ctx_rows, lat_len):
    nt = x.shape[0]
    tm = 2 * ROW_TILE
    return pl.pallas_call(
        _moe_kernel,
        grid=(nt // tm, N_EXPERTS),
        in_specs=[
            pl.BlockSpec((tm, D_MODEL), lambda i, e: (i, 0)),
            _mod_spec(tm, ctx_rows, lat_len),
            _const_spec((1, D_MODEL)), _const_spec((1, D_MODEL)),
            _const_spec((3, D_MODEL, LANES)), _const_spec((1, LANES)),
            pl.BlockSpec((None, D_MODEL, 2 * D_FF_EXPERT), lambda i, e: (e, 0, 0)),
            pl.BlockSpec((None, D_FF_EXPERT, D_MODEL), lambda i, e: (e, 0, 0)),
        ],
        out_specs=pl.BlockSpec((tm, D_MODEL), lambda i, e: (i, 0)),
        out_shape=jax.ShapeDtypeStruct((nt, D_MODEL), F32),
        scratch_shapes=[pltpu.VMEM((tm, D_MODEL), BF16), pltpu.VMEM((tm, LANES), F32),
                        pltpu.VMEM((tm, D_MODEL), F32)],
        compiler_params=_params(("parallel", "arbitrary")),
    )(x, mod, g2, g3, rw, rb, wgu, wd)


def _permute_w_in(w):
    offs = dict(zip(_IN_NAMES, np.cumsum((0,) + _IN_SIZES[:-1])))
    sizes = dict(zip(_IN_NAMES, _IN_SIZES))
    cols = [w[:, int(offs[n]):int(offs[n]) + sizes[n]] for n in _P_ORDER]
    wp = jnp.concatenate(cols, axis=-1)
    return jnp.pad(wp, ((0, 0), (0, P_COLS - wp.shape[1]))).astype(BF16)


def _block_ones(n, width):
    idx = np.arange(n) // width
    return jnp.asarray(idx[:, None] == idx[None, :], dtype=BF16)


def _trunk_layer(l, x, mod, ctx, w, dims):
    bc, tc, bl, tl = dims
    cr = bc * tc
    row1 = lambda a: a.reshape(1, -1)
    norm_g = w["norm_g"][l]
    p = _inproj(x, mod, row1(norm_g[0]), _permute_w_in(w["w_in"][l]), cr, tl)

    qg = row1(jnp.tile(w["qk_g"][l, 0], N_HEADS))
    kg = row1(jnp.tile(w["qk_g"][l, 1], N_KV_HEADS))
    bd = _block_ones(N_HEADS * HEAD_DIM, HEAD_DIM)
    qn_c, kn_c, k_ctx = _qkprep(p, 0, cr, qg, kg, bd, None)
    qn_l, kn_l = _qkprep(p, cr, bl * tl, qg, kg, bd, _rope_tables(tl))
    att_c = _attention_ctx(qn_c, kn_c, p, bc, tc)
    kw = N_KV_HEADS * HEAD_DIM
    att_l = _attention_lat(qn_l, kn_l, p, ctx["cache_k"][:, l].reshape(bl, -1, kw),
                           ctx["cache_v"][:, l].reshape(bl, -1, kw), bl, tl, cr // tl)
    att = jnp.concatenate([att_c, att_l], axis=0)
    v_ctx = p[:cr, _P["av"][0]:_P["av"][0] + kw]

    wa2 = w["gla_wa2"][l].astype(BF16)
    ba = w["gla_ba"][l].reshape(2, 1, -1)
    gla_c, gla_state = _gla(p, wa2, ba, None, bc, tc, 0, True)
    (gla_l,) = _gla(p, wa2, ba, ctx["state_gla"][:, l], bl, tl, cr // tl, False)
    gla_o = jnp.concatenate([gla_c, gla_l], axis=0)

    mats = _s5_matrices(w["s5_lam_re"][l], w["s5_lam_im"][l], w["s5_log_dt"][l], w["s5_b_re"][l],
                        w["s5_b_im"][l], w["s5_c_re"][l], w["s5_c_im"][l])
    sre, sim = ctx["state_s5_re"][:, l].astype(F32), ctx["state_s5_im"][:, l].astype(F32)
    x0 = jnp.stack([jnp.concatenate([sre, sim], axis=-1), jnp.concatenate([sim, sre], axis=-1)], axis=2)
    s5y, s5_fin = _s5(p, mats, x0, bc, tc, bl, tl)

    x = _merge(x, mod, p, gla_o, att, s5y, row1(w["gla_norm_g"][l]), row1(w["s5_d"][l]),
               w["s5_glu_w"][l].astype(BF16), row1(w["s5_glu_b"][l]), w["w_branch"][l].astype(BF16),
               w["w_out"][l].astype(BF16), row1(norm_g[1]), cr, tl)

    if l % 2 == 0:
        x = _ffn(x, mod, row1(norm_g[2]), row1(norm_g[3]), w["ffn_w_gu"][l // 2].astype(BF16),
                 w["ffn_w_down"][l // 2].astype(BF16), cr, tl)
    else:
        rw = jnp.pad(w["router_w"][l // 2], ((0, 0), (0, LANES - N_EXPERTS)))
        rw = jnp.stack(_split3(rw))
        rb = jnp.pad(w["router_b"][l // 2], (0, LANES - N_EXPERTS)).reshape(1, LANES)
        x = _moe(x, mod, row1(norm_g[2]), row1(norm_g[3]), rw, rb, w["moe_w_gu"][l // 2].astype(BF16),
                 w["moe_w_down"][l // 2].astype(BF16), cr, tl)
    return x, (k_ctx, v_ctx, gla_state, s5_fin)


def kernel(x_prompt, x_sample, cache_k, cache_v, state_gla, state_s5_re, state_s5_im, c, c_ctx, ada_w, ada_b, norm_g, w_in, gla_wa2, gla_ba, gla_norm_g, qk_g, s5_lam_re, s5_lam_im, s5_log_dt, s5_b_re, s5_b_im, s5_c_re, s5_c_im, s5_d, s5_glu_w, s5_glu_b, w_branch, w_out, ffn_w_gu, ffn_w_down, router_w, router_b, moe_w_gu, moe_w_down):
    bc, tc, _ = x_prompt.shape
    bl, tl, _ = x_sample.shape
    depth = w_in.shape[0]
    cr = bc * tc
    assert cr % tl == 0 and tc % ROW_TILE == 0 and tl % (2 * ROW_TILE) == 0 and 1 + bl <= 8
    dims = (bc, tc, bl, tl)
    w = dict(norm_g=norm_g, w_in=w_in, gla_wa2=gla_wa2, gla_ba=gla_ba, gla_norm_g=gla_norm_g, qk_g=qk_g,
             s5_lam_re=s5_lam_re, s5_lam_im=s5_lam_im, s5_log_dt=s5_log_dt, s5_b_re=s5_b_re, s5_b_im=s5_b_im,
             s5_c_re=s5_c_re, s5_c_im=s5_c_im, s5_d=s5_d, s5_glu_w=s5_glu_w, s5_glu_b=s5_glu_b,
             w_branch=w_branch, w_out=w_out, ffn_w_gu=ffn_w_gu, ffn_w_down=ffn_w_down, router_w=router_w,
             router_b=router_b, moe_w_gu=moe_w_gu, moe_w_down=moe_w_down)
    ctx = dict(cache_k=cache_k, cache_v=cache_v, state_gla=state_gla, state_s5_re=state_s5_re,
               state_s5_im=state_s5_im)

    cvec = jnp.concatenate([c_ctx[None], c, jnp.zeros((8 - 1 - bl, D_MODEL), F32)], axis=0)
    mods = _modulation(cvec, ada_w, ada_b)
    x = jnp.concatenate([x_prompt.reshape(cr, D_MODEL), x_sample.reshape(bl * tl, D_MODEL)], axis=0)
    ks, vs, glas, s5s = [], [], [], []
    for l in range(depth):
        x, (k_l, v_l, gla_l, s5_l) = _trunk_layer(l, x, mods[l], ctx, w, dims)
        ks.append(k_l.reshape(bc, tc, N_KV_HEADS, HEAD_DIM))
        vs.append(v_l.reshape(bc, tc, N_KV_HEADS, HEAD_DIM))
        glas.append(gla_l)
        s5s.append(s5_l)
    new_s5 = jnp.stack(s5s, axis=1)
    return (x[:cr].reshape(bc, tc, D_MODEL), x[cr:].reshape(bl, tl, D_MODEL),
            jnp.stack(ks, axis=1), jnp.stack(vs, axis=1), jnp.stack(glas, axis=1),
            new_s5[..., :S5_STATE], new_s5[..., S5_STATE:])
```
